```python
import math
import jax, jax.numpy as jnp
from jax import lax
import numpy as np

D_MODEL = 1024
BATCH = 8
SEQ = 2048
DEPTH = 4
DEC_BATCH = 128
DEC_SEQ = 8
PAST_LEN = 2048
PAGE_SIZE = 128

EXPAND = 2
D_INNER = EXPAND * D_MODEL
N_A_LAYERS = DEPTH // 2
N_B_LAYERS = DEPTH - N_A_LAYERS
RWKV_HEAD = 64
RWKV_HEADS = D_INNER // RWKV_HEAD
DECAY_LORA = 64
AAA_LORA = 64
DIFF_HEAD = 64
DIFF_HEADS = D_INNER // (2 * DIFF_HEAD)
DIFF_KV_HEADS = DIFF_HEADS // 2
DIFF_GROUP = DIFF_HEADS // DIFF_KV_HEADS
KV_WIDTH = DIFF_KV_HEADS * 2 * DIFF_HEAD
ROPE_DIMS = DIFF_HEAD // 4
ROPE_THETA = 500000.0
Q_BLOCK = 128
LN_EPS = 1e-5
GN_EPS = 64e-5
DEEPNORM_ALPHA = (2 * DEPTH) ** 0.25
DEEPNORM_BETA = (8 * DEPTH) ** -0.25
NEG_BIG = -1e30

kernel_name = "rwkv7_diffattn_yoco_step"


def _layer_norm(x, g, b):
    xf = x.astype(jnp.float32)
    mean = xf.mean(-1, keepdims=True)
    var = jnp.square(xf - mean).mean(-1, keepdims=True)
    return ((xf - mean) * lax.rsqrt(var + LN_EPS)).astype(x.dtype) * g + b


def _ada(c, w, b, n):
    m = jnp.einsum('bd,de->be', jax.nn.silu(c), w) + b
    return jnp.split(m[:, None, :], n, axis=-1)


def _rope(x, pos):
    half = ROPE_DIMS // 2
    inv = ROPE_THETA ** (-jnp.arange(half, dtype=jnp.float32) * 2.0 / ROPE_DIMS)
    ang = pos.astype(jnp.float32)[:, None] * inv[None, :]
    shape = (1, ang.shape[0]) + (1,) * (x.ndim - 3) + (half,)
    cos = jnp.cos(ang).reshape(shape).astype(x.dtype)
    sin = jnp.sin(ang).reshape(shape).astype(x.dtype)
    x1, x2 = x[..., :half], x[..., half:ROPE_DIMS]
    return jnp.concatenate([x1 * cos - x2 * sin, x2 * cos + x1 * sin, x[..., ROPE_DIMS:]], axis=-1)


def _rwkv7_mixer(h, shift0, wkv0, mu, w_rkvg, w0, w_decay1, w_decay2, a0, w_a1, w_a2,
                 k_k, k_a, r_k, gn_g, gn_b, w_o):
    B, T, _ = h.shape
    h_prev = jnp.concatenate([shift0[:, None, :].astype(h.dtype), h[:, :-1]], axis=1)
    xs = h[None] + (h_prev - h)[None] * mu[:, None, None, :]
    r, k, v, g = jnp.einsum('nbtd,nde->nbte', xs[:4], w_rkvg)
    w_log = -jax.nn.softplus(-(w0 + jnp.tanh(xs[4] @ w_decay1) @ w_decay2)) - 0.5
    decay = jnp.exp(-jnp.exp(w_log))
    a = jax.nn.sigmoid(a0 + (xs[5] @ w_a1) @ w_a2)
    heads = lambda t: t.reshape(B, T, RWKV_HEADS, RWKV_HEAD)
    kk = heads(k * k_k)
    kk = kk * lax.rsqrt(jnp.maximum(jnp.sum(jnp.square(kk.astype(jnp.float32)), -1, keepdims=True), 1e-24)).astype(kk.dtype)
    k = k * (1 + (a - 1) * k_a)
    r, decay, k, v, a = (heads(t) for t in (r, decay, k, v, a))
    seq = tuple(jnp.moveaxis(t, 1, 0) for t in (r, decay, k, v, -kk, kk * a))

    def step(S, inp):
        r_t, w_t, k_t, v_t, za_t, zb_t = inp
        sa = jnp.einsum('bhij,bhj->bhi', S, za_t)
        S = (S * w_t[:, :, None, :] + sa[..., None] * zb_t[:, :, None, :]
             + v_t[..., None] * k_t[:, :, None, :]).astype(S.dtype)
        return S, jnp.einsum('bhij,bhj->bhi', S, r_t)

    s_last, y = lax.scan(step, wkv0, seq)
    y = jnp.moveaxis(y, 0, 1).astype(jnp.float32)
    mean = y.mean(-1, keepdims=True)
    var = jnp.square(y - mean).mean(-1, keepdims=True)
    y = ((y - mean) * lax.rsqrt(var + GN_EPS)).astype(h.dtype).reshape(B, T, D_INNER) * gn_g + gn_b
    bonus = jnp.sum(r * k * r_k, axis=-1, keepdims=True) * v
    out = (y + bonus.reshape(B, T, D_INNER)) * jax.nn.silu(g)
    return out @ w_o, h[:, -1], s_last


def _shared_kv(x, c, ada_kv_w, ada_kv_b, w_kv, pos):
    B, T, _ = x.shape
    shift, scale = _ada(c, ada_kv_w, ada_kv_b, 2)
    k, v = jnp.split((x * (1 + scale) + shift) @ w_kv, 2, axis=-1)
    k = _rope(k.reshape(B, T, DIFF_KV_HEADS, 2, DIFF_HEAD), pos).reshape(B, T, DIFF_KV_HEADS, 2 * DIFF_HEAD)
    return k, v.reshape(B, T, DIFF_KV_HEADS, 2 * DIFF_HEAD)


def _diff_attn(q, k, v, q_pos, k_pos, lam):
    s = jnp.einsum('btkgmd,bskmd->bkgmts', q, k).astype(jnp.float32) * (DIFF_HEAD ** -0.5)
    mask = k_pos[None, :] <= q_pos[:, None]
    p = jax.nn.softmax(jnp.where(mask, s, NEG_BIG), axis=-1)
    amap = (p[:, :, :, 0] - lam * p[:, :, :, 1]).astype(v.dtype)
    return jnp.einsum('bkgts,bskv->btkgv', amap, v)


def _diff_mixer(h, k_all, v_all, q_pos, k_pos, blocked, lam_init, w_qg, lam_qk, subln_g, w_o):
    B, T, _ = h.shape
    q, gate = jnp.split(h @ w_qg, 2, axis=-1)
    q = _rope(q.reshape(B, T, DIFF_KV_HEADS, DIFF_GROUP, 2, DIFF_HEAD), q_pos)
    k = k_all.reshape(k_all.shape[0], k_all.shape[1], DIFF_KV_HEADS, 2, DIFF_HEAD)
    lq = lam_qk.astype(jnp.float32)
    lam = jnp.exp(jnp.sum(lq[0] * lq[1])) - jnp.exp(jnp.sum(lq[2] * lq[3])) + lam_init
    if blocked:
        outs = []
        for i in range(T // Q_BLOCK):
            lo, hi = i * Q_BLOCK, (i + 1) * Q_BLOCK
            outs.append(_diff_attn(q[:, lo:hi], k[:, :hi], v_all[:, :hi], q_pos[lo:hi], k_pos[:hi], lam))
        o = jnp.concatenate(outs, axis=1)
    else:
        o = _diff_attn(q, k, v_all, q_pos, k_pos, lam)
    of = o.astype(jnp.float32)
    o = (of * lax.rsqrt(jnp.mean(jnp.square(of), -1, keepdims=True) + LN_EPS)).astype(h.dtype) * subln_g
    o = (o * (1.0 - lam_init)).reshape(B, T, D_INNER) * jax.nn.silu(gate)
    return o @ w_o


def _run_group(x, c, pos, shift0, wkv0, past_k, past_v, blocked, P):
    new_shift, new_wkv = [], []
    k_new = v_new = k_all = v_all = k_pos = None
    for l in range(DEPTH):
        if l == N_A_LAYERS:
            k_new, v_new = _shared_kv(x, c, P['ada_kv_w'], P['ada_kv_b'], P['w_kv'], pos)
            if past_k is None:
                k_all, v_all, k_pos = k_new, v_new, pos
            else:
                k_all = jnp.concatenate([past_k.astype(k_new.dtype), k_new], axis=1)
                v_all = jnp.concatenate([past_v.astype(v_new.dtype), v_new], axis=1)
                k_pos = jnp.arange(k_all.shape[1], dtype=jnp.int32)
        shift, scale, gate = _ada(c, P['ada_w'][l], P['ada_b'][l], 3)
        h = x * (1 + scale) + shift
        if l < N_A_LAYERS:
            out, last_h, s_last = _rwkv7_mixer(
                h, shift0[l], wkv0[l], P['mu'][l], P['w_rkvg'][l], P['w0'][l], P['w_decay1'][l],
                P['w_decay2'][l], P['a0'][l], P['w_a1'][l], P['w_a2'][l], P['k_k'][l], P['k_a'][l],
                P['r_k'][l], P['gn_g'][l], P['gn_b'][l], P['w_o_a'][l])
            new_shift.append(last_h)
            new_wkv.append(s_last)
        else:
            j = l - N_A_LAYERS
            lam_init = 0.8 - 0.6 * math.exp(-0.3 * l)
            out = _diff_mixer(h, k_all, v_all, pos, k_pos, blocked, lam_init, P['w_qg'][j],
                              P['lam_qk'][j], P['subln_g'][j], P['w_o_b'][j])
        x = _layer_norm(DEEPNORM_ALPHA * x + (1 + gate) * out, P['ln_g'][l], P['ln_b'][l])
    return x, k_new, v_new, jnp.stack(new_wkv), jnp.stack(new_shift)


def setup_inputs(seed: int = 0) -> dict:
    key = jax.random.key(seed)
    ks = jax.random.split(key, 36)
    nrm = lambda i, shape, s: jax.random.normal(ks[i], shape, jnp.float32) * s
    n_pages = PAST_LEN // PAGE_SIZE
    n_used = DEC_BATCH * n_pages
    n_phys = n_used + max(1, n_used // 4)
    page_table = jax.random.permutation(ks[4], n_phys)[:n_used].reshape(DEC_BATCH, n_pages).astype(jnp.int32)
    D, E = D_MODEL, D_INNER
    ada_s = 0.3 * D ** -0.5
    v_scale = jnp.array([1.0, 1.0, DEEPNORM_BETA, 1.0], jnp.float32)[None, :, None, None]
    return {
        'x_prompt': nrm(0, (BATCH, SEQ, D), 1.0),
        'x_sample': nrm(1, (DEC_BATCH, DEC_SEQ, D), 1.0),
        'cache_k': nrm(2, (n_phys, PAGE_SIZE, DIFF_KV_HEADS, 2 * DIFF_HEAD), 1.0),
        'cache_v': nrm(3, (n_phys, PAGE_SIZE, DIFF_KV_HEADS, 2 * DIFF_HEAD), DEEPNORM_BETA),
        'page_table': page_table,
        'state_wkv': nrm(5, (N_A_LAYERS, DEC_BATCH, RWKV_HEADS, RWKV_HEAD, RWKV_HEAD), 0.3),
        'state_shift': nrm(6, (N_A_LAYERS, DEC_BATCH, D), 1.0),
        'c_prompt': nrm(7, (BATCH, D), 1.0),
        'c_sample': nrm(8, (DEC_BATCH, D), 1.0),
        'ada_w': nrm(9, (DEPTH, D, 3 * D), ada_s),
        'ada_b': nrm(10, (DEPTH, 3 * D), 0.02),
        'ln_g': 1.0 + nrm(11, (DEPTH, D), 0.05),
        'ln_b': nrm(12, (DEPTH, D), 0.02),
        'mu': jax.random.uniform(ks[13], (N_A_LAYERS, 6, D), jnp.float32),
        'w_rkvg': nrm(14, (N_A_LAYERS, 4, D, E), D ** -0.5) * v_scale,
        'w0': jax.random.uniform(ks[15], (N_A_LAYERS, E), jnp.float32, minval=-4.0, maxval=1.0),
        'w_decay1': nrm(16, (N_A_LAYERS, D, DECAY_LORA), D ** -0.5),
        'w_decay2': nrm(17, (N_A_LAYERS, DECAY_LORA, E), 0.1),
        'a0': nrm(18, (N_A_LAYERS, E), 0.5),
        'w_a1': nrm(19, (N_A_LAYERS, D, AAA_LORA), D ** -0.5),
        'w_a2': nrm(20, (N_A_LAYERS, AAA_LORA, E), 0.1),
        'k_k': 0.85 + nrm(21, (N_A_LAYERS, E), 0.05),
        'k_a': 1.0 + nrm(22, (N_A_LAYERS, E), 0.05),
        'r_k': nrm(23, (N_A_LAYERS, RWKV_HEADS, RWKV_HEAD), 0.1),
        'gn_g': 1.0 + nrm(24, (N_A_LAYERS, E), 0.05),
        'gn_b': nrm(25, (N_A_LAYERS, E), 0.02),
        'w_o_a': nrm(26, (N_A_LAYERS, E, D), E ** -0.5 * DEEPNORM_BETA),
        'ada_kv_w': nrm(27, (D, 2 * D), ada_s),
        'ada_kv_b': nrm(28, (2 * D,), 0.02),
        'w_kv': jnp.concatenate([nrm(29, (D, KV_WIDTH), D ** -0.5),
                                 nrm(30, (D, KV_WIDTH), D ** -0.5 * DEEPNORM_BETA)], axis=1),
        'w_qg': nrm(31, (N_B_LAYERS, D, 2 * E), D ** -0.5),
        'lam_qk': nrm(32, (N_B_LAYERS, 4, DIFF_HEAD), 0.1),
        'subln_g': 1.0 + nrm(33, (N_B_LAYERS, 2 * DIFF_HEAD), 0.05),
        'w_o_b': nrm(34, (N_B_LAYERS, E, D), E ** -0.5 * DEEPNORM_BETA),
    }


def reference(x_prompt, x_sample, cache_k, cache_v, page_table, state_wkv, state_shift, c_prompt, c_sample,
              ada_w, ada_b, ln_g, ln_b, mu, w_rkvg, w0, w_decay1, w_decay2, a0, w_a1, w_a2, k_k, k_a, r_k,
              gn_g, gn_b, w_o_a, ada_kv_w, ada_kv_b, w_kv, w_qg, lam_qk, subln_g, w_o_b):
    P = {'ada_w': ada_w, 'ada_b': ada_b, 'ln_g': ln_g, 'ln_b': ln_b, 'mu': mu, 'w_rkvg': w_rkvg, 'w0': w0,
         'w_decay1': w_decay1, 'w_decay2': w_decay2, 'a0': a0, 'w_a1': w_a1, 'w_a2': w_a2, 'k_k': k_k,
         'k_a': k_a, 'r_k': r_k, 'gn_g': gn_g, 'gn_b': gn_b, 'w_o_a': w_o_a, 'ada_kv_w': ada_kv_w,
         'ada_kv_b': ada_kv_b, 'w_kv': w_kv, 'w_qg': w_qg, 'lam_qk': lam_qk, 'subln_g': subln_g, 'w_o_b': w_o_b}
    B, T, _ = x_prompt.shape
    pos_p = jnp.arange(T, dtype=jnp.int32)
    shift0_p = jnp.zeros((N_A_LAYERS, B, D_MODEL), x_prompt.dtype)
    wkv0_p = jnp.zeros((N_A_LAYERS, B, RWKV_HEADS, RWKV_HEAD, RWKV_HEAD), x_prompt.dtype)
    y_prompt, k_prompt, v_prompt, wkv_prompt, shift_prompt = _run_group(
        x_prompt, c_prompt, pos_p, shift0_p, wkv0_p, None, None, True, P)
    DB, TS, _ = x_sample.shape
    past_len = page_table.shape[1] * cache_k.shape[1]
    past_k = cache_k[page_table].reshape(DB, past_len, DIFF_KV_HEADS, 2 * DIFF_HEAD)
    past_v = cache_v[page_table].reshape(DB, past_len, DIFF_KV_HEADS, 2 * DIFF_HEAD)
    pos_s = past_len + jnp.arange(TS, dtype=jnp.int32)
    y_sample, k_sample, v_sample, wkv_sample, shift_sample = _run_group(
        x_sample, c_sample, pos_s, state_shift, state_wkv, past_k, past_v, False, P)
    return (y_prompt, y_sample, k_prompt, v_prompt, k_sample, v_sample, wkv_prompt, shift_prompt, wkv_sample, shift_sample)
```

```python
import functools
import math

import jax
import jax.numpy as jnp
from jax import lax
from jax.experimental import pallas as pl
from jax.experimental.pallas import tpu as pltpu

F32 = jnp.float32
BF16 = jnp.bfloat16
HIGHEST = lax.Precision.HIGHEST

RWKV_HEAD = 64
DIFF_HEAD = 64
ROPE_DIMS = DIFF_HEAD // 4
ROPE_THETA = 500000.0
LN_EPS = 1e-5
GN_EPS = 64e-5
NEG_BIG = -1e30
LANES = 128
SUBLANES = 8
VMEM_LIMIT = 56 * 1024 * 1024


def _silu(x):
    return x / (1.0 + jnp.exp(-x))


def _sigmoid(x):
    return 1.0 / (1.0 + jnp.exp(-x))


def _softplus(x):
    return jnp.maximum(x, 0.0) + jnp.log(1.0 + jnp.exp(-jnp.abs(x)))


def _dot(a, b, precision=None):
    return jnp.dot(a, b, precision=precision, preferred_element_type=F32)


def _dot_nt(a, b, precision=None):
    return lax.dot_general(a, b, (((1,), (1,)), ((), ())), precision=precision, preferred_element_type=F32)


def _dot_tn(a, b, precision=None):
    return lax.dot_general(a, b, (((0,), (0,)), ((), ())), precision=precision, preferred_element_type=F32)


def _params(*sem):
    return pltpu.CompilerParams(dimension_semantics=sem, vmem_limit_bytes=VMEM_LIMIT)


class _Rows:
    def __init__(self, nb, R, tm, mod_base):
        self.nb, self.R, self.tm, self.mod_base = nb, R, tm, mod_base
        if R >= tm:
            assert R % tm == 0
            self.tiles_per_batch, self.bb = R // tm, 1
        else:
            assert tm % R == 0 and (nb * R) % tm == 0 and mod_base % (tm // R) == 0
            self.tiles_per_batch, self.bb = 0, tm // R
        self.n_tiles = nb * R // tm

    def mod_block(self, i):
        if self.bb == 1:
            return self.mod_base + i // self.tiles_per_batch
        return self.mod_base // self.bb + i

    def expand(self, m):
        if self.bb == 1:
            return m[0]
        W = m.shape[-1]
        return jnp.broadcast_to(m, (self.bb, self.R, W)).reshape(self.tm, W)


def _ada_kernel(c_ref, w_ref, b_ref, o_ref):
    o_ref[0] = _dot(_silu(c_ref[...]), w_ref[0], HIGHEST) + b_ref[0]


def _ada(c_all, w, b):
    L, D, n = w.shape
    nb = c_all.shape[0]
    tn = 512
    return pl.pallas_call(
        _ada_kernel, grid=(L, n // tn),
        in_specs=[pl.BlockSpec((nb, D), lambda l, j: (0, 0)),
                  pl.BlockSpec((1, D, tn), lambda l, j: (l, 0, j)),
                  pl.BlockSpec((1, 1, tn), lambda l, j: (l, 0, j))],
        out_specs=pl.BlockSpec((1, nb, tn), lambda l, j: (l, 0, j)),
        out_shape=jax.ShapeDtypeStruct((L, nb, n), F32),
        compiler_params=_params("parallel", "parallel"), name="ada_mod",
    )(c_all, w, b.reshape(L, 1, n))


def _modrow_kernel(x_ref, sh_ref, sc_ref, o_ref):
    o_ref[...] = x_ref[...] * (1.0 + sc_ref[...]) + sh_ref[...]


def _mod_rows(x_rows, shift, scale):
    return pl.pallas_call(_modrow_kernel, out_shape=jax.ShapeDtypeStruct(x_rows.shape, F32), name="mod_rows")(
        x_rows, shift, scale)


def _group_sum(x, ones_bd):
    W = x.shape[-1]
    hi = x.astype(BF16)
    lo = (x - hi.astype(F32)).astype(BF16)
    cols = []
    for j in range(W // LANES):
        sl = slice(j * LANES, (j + 1) * LANES)
        cols.append(_dot(hi[:, sl], ones_bd) + _dot(lo[:, sl], ones_bd))
    return cols[0] if len(cols) == 1 else jnp.concatenate(cols, axis=-1)


def _proj_kernel(x_ref, xp_ref, sh_ref, sc_ref, s0_ref, mu_ref, w_ref, w0_ref, wd1_ref, wd2_ref, a0_ref, wa1_ref,
                 wa2_ref, kk_ref, ka_ref, ones_ref,
                 r_o, lw_o, k_o, v_o, za_o, zb_o, sg_o, *, rows):
    i = pl.program_id(1)
    tm, R = rows.tm, rows.R
    x = x_ref[...]
    sh, sc = rows.expand(sh_ref[...]), rows.expand(sc_ref[...])
    h = x * (1.0 + sc) + sh
    rid = lax.broadcasted_iota(jnp.int32, (tm, 1), 0)
    rolled = pltpu.roll(h, 1, 0)
    if rows.bb == 1:
        prev_last = xp_ref[0][SUBLANES - 1:SUBLANES, :] * (1.0 + sc) + sh
        first_tile = (i % rows.tiles_per_batch) == 0
        row0 = jnp.where(first_tile, s0_ref[0], prev_last)
        h_prev = jnp.where(rid == 0, row0, rolled)
    else:
        h_prev = jnp.where(rid % R == 0, rows.expand(s0_ref[...]), rolled)
    d = h_prev - h
    mu = mu_ref[...]

    def mix(n):
        return (h + d * mu[n:n + 1, :]).astype(BF16)

    r = _dot(mix(0), w_ref[0])
    k = _dot(mix(1), w_ref[1])
    v = _dot(mix(2), w_ref[2])
    g = _dot(mix(3), w_ref[3])
    lora_w = jnp.tanh(_dot(mix(4), wd1_ref[...])).astype(BF16)
    w_log = -_softplus(-(w0_ref[...] + _dot(lora_w, wd2_ref[...]))) - 0.5
    lora_a = _dot(mix(5), wa1_ref[...]).astype(BF16)
    a = _sigmoid(a0_ref[...] + _dot(lora_a, wa2_ref[...]))
    kk = k * kk_ref[...]
    ss = _group_sum(kk * kk, ones_ref[...])
    kk = kk * lax.rsqrt(jnp.maximum(ss, 1e-24))
    r_o[...] = r
    lw_o[...] = -jnp.exp(w_log)
    k_o[...] = k * (1.0 + (a - 1.0) * ka_ref[...])
    v_o[...] = v
    za_o[...] = -kk
    zb_o[...] = kk * a
    sg_o[...] = _silu(g)


def _rwkv_proj(x, mods, s0, l, P, rows, ones_bd):
    N, D = x.shape
    E = P["w_rkvg"].shape[-1]
    te = min(E, 1024)
    tm = rows.tm
    lora = P["w_decay1"].shape[-1]
    xp = x.reshape(N // SUBLANES, SUBLANES, D)
    tps = tm // SUBLANES
    if rows.bb == 1:
        s0_spec = pl.BlockSpec((1, 1, D), lambda j, i: (i // rows.tiles_per_batch, 0, 0))
    else:
        s0_spec = pl.BlockSpec((rows.bb, 1, D), lambda j, i: (i, 0, 0))
    vec = lambda: pl.BlockSpec((None, 1, te), lambda j, i: (l, 0, j))
    in_specs = [
        pl.BlockSpec((tm, D), lambda j, i: (i, 0)),
        pl.BlockSpec((1, SUBLANES, D), lambda j, i: (jnp.maximum(i * tps - 1, 0), 0, 0)),
        pl.BlockSpec((None, rows.bb, 1, D), lambda j, i: (l, rows.mod_block(i), 0, 0)),
        pl.BlockSpec((None, rows.bb, 1, D), lambda j, i: (l, rows.mod_block(i), 0, 1)),
        s0_spec,
        pl.BlockSpec((None, 6, D), lambda j, i: (l, 0, 0)),
        pl.BlockSpec((None, 4, D, te), lambda j, i: (l, 0, 0, j)),
        vec(),
        pl.BlockSpec((None, D, lora), lambda j, i: (l, 0, 0)),
        pl.BlockSpec((None, lora, te), lambda j, i: (l, 0, j)),
        vec(),
        pl.BlockSpec((None, D, lora), lambda j, i: (l, 0, 0)),
        pl.BlockSpec((None, lora, te), lambda j, i: (l, 0, j)),
        vec(), vec(),
        pl.BlockSpec((LANES, LANES), lambda j, i: (0, 0)),
    ]
    out_spec = pl.BlockSpec((tm, te), lambda j, i: (i, j))
    return pl.pallas_call(
        functools.partial(_proj_kernel, rows=rows), grid=(E // te, rows.n_tiles),
        in_specs=in_specs, out_specs=[out_spec] * 7,
        out_shape=[jax.ShapeDtypeStruct((N, E), F32)] * 7,
        compiler_params=_params("parallel", "parallel"), name="rwkv_proj",
    )(x, xp, mods, mods, s0, P["mu"], P["w_rkvg"], P["w0"], P["w_decay1"], P["w_decay2"], P["a0"], P["w_a1"],
      P["w_a2"], P["k_k"], P["k_a"], ones_bd)


def _expand_groups(x, n_groups, group):
    lane_g = lax.broadcasted_iota(jnp.int32, (1, x.shape[-1]), 1) // group
    return jnp.concatenate([jnp.where(lane_g == g, x, 0.0) for g in range(n_groups)], axis=0)


def _wkv_chunk(r, lw, k, v, za, zb, S, *, C, hp):
    L = hp * RWKV_HEAD
    hc = hp * C
    row = lax.broadcasted_iota(jnp.int32, (C, C), 0)
    col = lax.broadcasted_iota(jnp.int32, (C, C), 1)
    cum = _dot(jnp.where(col <= row, 1.0, 0.0), lw, HIGHEST)
    cl = cum[C - 1:C, :]
    e_in, e_ex, e_neg, e_last = jnp.exp(cum), jnp.exp(cum - lw), jnp.exp(-cum), jnp.exp(cl - cum)
    ar = jnp.concatenate([za * e_ex, r * e_in], axis=0)
    ebk = jnp.concatenate([_expand_groups(zb * e_neg, hp, RWKV_HEAD), _expand_groups(k * e_neg, hp, RWKV_HEAD)], axis=0)
    G = _dot_nt(ar, ebk, HIGHEST)
    prow = lax.broadcasted_iota(jnp.int32, (C, hc), 0)
    pcol = lax.broadcasted_iota(jnp.int32, (C, hc), 1) % C
    strict, incl = pcol < prow, pcol <= prow
    a_ab = jnp.where(strict, G[:C, :hc], 0.0)
    a_ak = jnp.where(strict, G[:C, hc:], 0.0)
    a_rb = jnp.where(incl, G[C:, :hc], 0.0)
    a_rk = jnp.where(incl, G[C:, hc:], 0.0)
    xy = _dot_nt(ar, S, HIGHEST) + _dot(jnp.concatenate([a_ak, a_rk], axis=0), _expand_groups(v, hp, RWKV_HEAD), HIGHEST)
    sa, yp = xy[:C], xy[C:]
    n_steps = int(math.log2(C))
    pw = a_ab
    for step in range(n_steps):
        sa = sa + _dot(pw, _expand_groups(sa, hp, RWKV_HEAD), HIGHEST)
        if step + 1 < n_steps:
            pw = _dot(pw, _expand_groups(pw, hp, C), HIGHEST)
    y = yp + _dot(a_rb, _expand_groups(sa, hp, RWKV_HEAD), HIGHEST)
    dS = _dot_tn(jnp.concatenate([sa, v], axis=0), jnp.concatenate([zb * e_last, k * e_last], axis=0), HIGHEST)
    srow = lax.broadcasted_iota(jnp.int32, (L, L), 0) // RWKV_HEAD
    scol = lax.broadcasted_iota(jnp.int32, (L, L), 1) // RWKV_HEAD
    S_new = S * jnp.exp(cl) + jnp.where(srow == scol, dS, 0.0)
    return y, S_new


def _scan_kernel(*refs, C, hp, n_chunks, t_rows, has_init):
    if has_init:
        (r_ref, lw_ref, k_ref, v_ref, za_ref, zb_ref, sg_ref, rk_ref, gg_ref, gb_ref, s0_ref,
         o_ref, sf_ref, S_scr) = refs
    else:
        (r_ref, lw_ref, k_ref, v_ref, za_ref, zb_ref, sg_ref, rk_ref, gg_ref, gb_ref,
         o_ref, sf_ref, S_scr) = refs
    t = pl.program_id(2)
    L = hp * RWKV_HEAD
    H = RWKV_HEAD

    @pl.when(t == 0)
    def _init():
        S_scr[...] = jnp.zeros((L, L), F32)
        if has_init:
            for hh in range(hp):
                S_scr[hh * H:(hh + 1) * H, hh * H:(hh + 1) * H] = s0_ref[hh]

    orow = lax.broadcasted_iota(jnp.int32, (LANES, LANES), 0) // H
    ocol = lax.broadcasted_iota(jnp.int32, (LANES, LANES), 1) // H
    ones_bd = jnp.where(orow == ocol, 1.0, 0.0).astype(BF16)
    rk, gg, gb = rk_ref[...], gg_ref[...], gb_ref[...]

    def load(ref, c):
        if t_rows < C:
            return jnp.concatenate([ref[...], jnp.zeros((C - t_rows, L), F32)], axis=0)
        return ref[c * C:(c + 1) * C, :]

    for c in range(n_chunks):
        r, lw, k, v = load(r_ref, c), load(lw_ref, c), load(k_ref, c), load(v_ref, c)
        y, S_new = _wkv_chunk(r, lw, k, v, load(za_ref, c), load(zb_ref, c), S_scr[...], C=C, hp=hp)
        S_scr[...] = S_new
        mean = _group_sum(y, ones_bd) * (1.0 / H)
        yc = y - mean
        var = _group_sum(yc * yc, ones_bd) * (1.0 / H)
        yn = yc * lax.rsqrt(var + GN_EPS) * gg + gb
        bonus = _group_sum(r * k * rk, ones_bd) * v
        out = (yn + bonus) * load(sg_ref, c)
        if t_rows < C:
            o_ref[...] = out[:t_rows, :]
        else:
            o_ref[c * C:(c + 1) * C, :] = out

    @pl.when(t == pl.num_programs(2) - 1)
    def _fin():
        for hh in range(hp):
            sf_ref[hh] = S_scr[hh * H:(hh + 1) * H, hh * H:(hh + 1) * H]


def _group_sum_wide(x, ones_bd):
    return _group_sum(x, ones_bd)


def _rwkv_scan(proj, wkv0, l, P, nb, R):
    N, E = proj[0].shape
    H = E // RWKV_HEAD
    hp, C = 2, 64
    Lw = hp * RWKV_HEAD
    tb = min(R, 256)
    t_rows = tb
    n_chunks = max(tb // C, 1)
    nt = R // tb
    has_init = wkv0 is not None
    seq = pl.BlockSpec((tb, Lw), lambda b, hg, t: (b * nt + t, hg))
    vec = pl.BlockSpec((None, 1, Lw), lambda b, hg, t: (l, 0, hg))
    in_specs = [seq] * 7 + [vec] * 3
    args = list(proj) + [P["r_k"], P["gn_g"], P["gn_b"]]
    if has_init:
        in_specs.append(pl.BlockSpec((None, None, hp, RWKV_HEAD, RWKV_HEAD), lambda b, hg, t: (l, b, hg, 0, 0)))
        args.append(wkv0)
    return pl.pallas_call(
        functools.partial(_scan_kernel, C=C, hp=hp, n_chunks=n_chunks, t_rows=t_rows, has_init=has_init),
        grid=(nb, H // hp, nt), in_specs=in_specs,
        out_specs=[seq, pl.BlockSpec((None, hp, RWKV_HEAD, RWKV_HEAD), lambda b, hg, t: (b, hg, 0, 0))],
        out_shape=[jax.ShapeDtypeStruct((N, E), F32), jax.ShapeDtypeStruct((nb, H, RWKV_HEAD, RWKV_HEAD), F32)],
        scratch_shapes=[pltpu.VMEM((Lw, Lw), F32)],
        compiler_params=_params("parallel", "parallel", "arbitrary"), name="rwkv_scan",
    )(*args)


def _outproj_kernel(o_ref, x_ref, gate_ref, w_ref, g_ref, b_ref, y_ref, *, rows, alpha):
    p = _dot(o_ref[...].astype(BF16), w_ref[...])
    z = alpha * x_ref[...] + (1.0 + rows.expand(gate_ref[...])) * p
    mean = jnp.mean(z, axis=-1, keepdims=True)
    zc = z - mean
    var = jnp.mean(zc * zc, axis=-1, keepdims=True)
    y_ref[...] = zc * lax.rsqrt(var + LN_EPS) * g_ref[...] + b_ref[...]


def _outproj(o, x, mods, w_o, lw, ln_g, ln_b, l, rows, alpha):
    N, E = o.shape
    D = x.shape[1]
    tm = rows.tm
    return pl.pallas_call(
        functools.partial(_outproj_kernel, rows=rows, alpha=alpha), grid=(rows.n_tiles,),
        in_specs=[pl.BlockSpec((tm, E), lambda i: (i, 0)),
                  pl.BlockSpec((tm, D), lambda i: (i, 0)),
                  pl.BlockSpec((None, rows.bb, 1, D), lambda i: (l, rows.mod_block(i), 0, 2)),
                  pl.BlockSpec((None, E, D), lambda i: (lw, 0, 0)),
                  pl.BlockSpec((None, 1, D), lambda i: (l, 0, 0)),
                  pl.BlockSpec((None, 1, D), lambda i: (l, 0, 0))],
        out_specs=pl.BlockSpec((tm, D), lambda i: (i, 0)),
        out_shape=jax.ShapeDtypeStruct((N, D), F32),
        compiler_params=_params("parallel"), name="out_proj_ln",
    )(o, x, mods, w_o, ln_g, ln_b)


def _rope_tables(pos, n_rows):
    half = ROPE_DIMS // 2
    inv = ROPE_THETA ** (-jnp.arange(half, dtype=F32) * 2.0 / ROPE_DIMS)
    ang = pos.astype(F32)[:, None] * inv[None, :]
    cos, sin = jnp.cos(ang), jnp.sin(ang)
    T = pos.shape[0]
    rest = DIFF_HEAD - ROPE_DIMS
    c64 = jnp.concatenate([cos, cos, jnp.ones((T, rest), F32)], axis=1)
    l64 = jnp.concatenate([-sin, jnp.zeros((T, half + rest), F32)], axis=1)
    r64 = jnp.concatenate([jnp.zeros((T, half), F32), sin, jnp.zeros((T, rest), F32)], axis=1)
    tabs = [jnp.tile(t, (n_rows // T, LANES // DIFF_HEAD)) for t in (c64, l64, r64)]
    return tabs


def _rope(x, cos, sin_lo, sin_hi):
    W = x.shape[-1]
    half = ROPE_DIMS // 2
    rep = W // LANES
    tile = lambda t: jnp.concatenate([t] * rep, axis=-1)
    return x * tile(cos) + pltpu.roll(x, W - half, 1) * tile(sin_lo) + pltpu.roll(x, half, 1) * tile(sin_hi)


def _tab_spec(rows, n_tab_rows):
    tm = rows.tm
    if n_tab_rows == tm:
        return pl.BlockSpec((tm, LANES), lambda i: (0, 0))
    per = n_tab_rows // tm
    return pl.BlockSpec((tm, LANES), lambda i: (i % per, 0))


def _kv_kernel(x_ref, sh_ref, sc_ref, w_ref, c_ref, sl_ref, sr_ref, k_o, v_o, *, rows):
    h = (x_ref[...] * (1.0 + rows.expand(sc_ref[...])) + rows.expand(sh_ref[...])).astype(BF16)
    kv = _dot(h, w_ref[...])
    W = kv.shape[-1] // 2
    k_o[...] = _rope(kv[:, :W], c_ref[...], sl_ref[...], sr_ref[...])
    v_o[...] = kv[:, W:]


def _shared_kv(x, mods_kv, w_kv, tabs, rows):
    N, D = x.shape
    W = w_kv.shape[1] // 2
    tm = rows.tm
    tab = _tab_spec(rows, tabs[0].shape[0])
    return pl.pallas_call(
        functools.partial(_kv_kernel, rows=rows), grid=(rows.n_tiles,),
        in_specs=[pl.BlockSpec((tm, D), lambda i: (i, 0)),
                  pl.BlockSpec((None, rows.bb, 1, D), lambda i: (0, rows.mod_block(i), 0, 0)),
                  pl.BlockSpec((None, rows.bb, 1, D), lambda i: (0, rows.mod_block(i), 0, 1)),
                  pl.BlockSpec((D, 2 * W), lambda i: (0, 0)), tab, tab, tab],
        out_specs=[pl.BlockSpec((tm, W), lambda i: (i, 0))] * 2,
        out_shape=[jax.ShapeDtypeStruct((N, W), F32)] * 2,
        compiler_params=_params("parallel"), name="shared_kv",
    )(x, mods_kv, mods_kv, w_kv, *tabs)


def _qg_kernel(x_ref, sh_ref, sc_ref, w_ref, c_ref, sl_ref, sr_ref, q_o, sg_o, *, rows):
    h = (x_ref[...] * (1.0 + rows.expand(sc_ref[...])) + rows.expand(sh_ref[...])).astype(BF16)
    qg = _dot(h, w_ref[...])
    E = qg.shape[-1] // 2
    q_o[...] = _rope(qg[:, :E], c_ref[...], sl_ref[...], sr_ref[...]) * (DIFF_HEAD ** -0.5)
    sg_o[...] = _silu(qg[:, E:])


def _qg_proj(x, mods, w_qg, j, l, tabs, rows):
    N, D = x.shape
    E = w_qg.shape[-1] // 2
    tm = rows.tm
    tab = _tab_spec(rows, tabs[0].shape[0])
    return pl.pallas_call(
        functools.partial(_qg_kernel, rows=rows), grid=(rows.n_tiles,),
        in_specs=[pl.BlockSpec((tm, D), lambda i: (i, 0)),
                  pl.BlockSpec((None, rows.bb, 1, D), lambda i: (l, rows.mod_block(i), 0, 0)),
                  pl.BlockSpec((None, rows.bb, 1, D), lambda i: (l, rows.mod_block(i), 0, 1)),
                  pl.BlockSpec((None, D, 2 * E), lambda i: (j, 0, 0)), tab, tab, tab],
        out_specs=[pl.BlockSpec((tm, E), lambda i: (i, 0))] * 2,
        out_shape=[jax.ShapeDtypeStruct((N, E), F32)] * 2,
        compiler_params=_params("parallel"), name="qg_proj",
    )(x, mods, mods, w_qg, *tabs)


def _lam(lq, lam_init):
    a = jnp.sum(lq[0:1, :] * lq[1:2, :], axis=-1, keepdims=True)
    b = jnp.sum(lq[2:3, :] * lq[3:4, :], axis=-1, keepdims=True)
    return jnp.exp(a) - jnp.exp(b) + lam_init


def _online_update(s, vb, m_ref, l_ref, acc_ref, idx):
    m_old = m_ref[idx][:, 0:1]
    m_new = jnp.maximum(m_old, jnp.max(s, axis=-1, keepdims=True))
    p = jnp.exp(s - m_new)
    corr = jnp.exp(m_old - m_new)
    l_new = corr * l_ref[idx][:, 0:1] + jnp.sum(p, axis=-1, keepdims=True)
    acc_ref[idx] = corr * acc_ref[idx] + _dot(p.astype(BF16), vb)
    m_ref[idx] = jnp.broadcast_to(m_new, m_ref.shape[1:])
    l_ref[idx] = jnp.broadcast_to(l_new, l_ref.shape[1:])


def _finish_head(o1, o2, lam, sub_g, sg, lam_init):
    o = o1 - lam * o2
    ms = jnp.mean(o * o, axis=-1, keepdims=True)
    return o * lax.rsqrt(ms + LN_EPS) * sub_g * (1.0 - lam_init) * sg


def _attn_p_kernel(lq_ref, q_ref, k_ref, v_ref, sg_ref, sub_ref, o_ref, m_scr, l_scr, acc_scr, *, tq, lam_init):
    i = pl.program_id(2)
    V = 2 * DIFF_HEAD
    m_scr[...] = jnp.full(m_scr.shape, NEG_BIG, F32)
    l_scr[...] = jnp.zeros(l_scr.shape, F32)
    acc_scr[...] = jnp.zeros(acc_scr.shape, F32)
    lane = lax.broadcasted_iota(jnp.int32, (1, V), 1)
    row = lax.broadcasted_iota(jnp.int32, (tq, tq), 0)
    col = lax.broadcasted_iota(jnp.int32, (tq, tq), 1)
    q = q_ref[...]
    qs = []
    for g in range(2):
        qg = q[:, g * V:(g + 1) * V]
        for m in range(2):
            qs.append(jnp.where((lane < DIFF_HEAD) == (m == 0), qg, 0.0).astype(BF16))

    def body(j, carry):
        start = pl.multiple_of(j * tq, tq)
        kb = k_ref[pl.ds(start, tq), :].astype(BF16)
        vb = v_ref[pl.ds(start, tq), :].astype(BF16)
        visible = (col + j * tq) <= (row + i * tq)
        for idx in range(4):
            s = jnp.where(visible, _dot_nt(qs[idx], kb), NEG_BIG)
            _online_update(s, vb, m_scr, l_scr, acc_scr, idx)
        return carry

    lax.fori_loop(0, i + 1, body, 0)
    lam = _lam(lq_ref[...], lam_init)
    sg = sg_ref[...]
    for g in range(2):
        o1 = acc_scr[2 * g] / l_scr[2 * g][:, 0:1]
        o2 = acc_scr[2 * g + 1] / l_scr[2 * g + 1][:, 0:1]
        o_ref[:, g * V:(g + 1) * V] = _finish_head(o1, o2, lam, sub_ref[...], sg[:, g * V:(g + 1) * V], lam_init)


def _attn_prompt(q, k, v, sg, lam_qk, subln_g, j, lam_init, B, T):
    N, E = q.shape
    V = 2 * DIFF_HEAD
    KVH = k.shape[1] // V
    tq = min(T, 256)
    nq = T // tq
    qspec = pl.BlockSpec((tq, 2 * V), lambda b, h, i: (b * nq + i, h))
    kspec = pl.BlockSpec((T, V), lambda b, h, i: (b, h))
    return pl.pallas_call(
        functools.partial(_attn_p_kernel, tq=tq, lam_init=lam_init), grid=(B, KVH, nq),
        in_specs=[pl.BlockSpec((None, 4, DIFF_HEAD), lambda b, h, i: (j, 0, 0)), qspec, kspec, kspec, qspec,
                  pl.BlockSpec((None, 1, V), lambda b, h, i: (j, 0, 0))],
        out_specs=qspec, out_shape=jax.ShapeDtypeStruct((N, E), F32),
        scratch_shapes=[pltpu.VMEM((4, tq, LANES), F32), pltpu.VMEM((4, tq, LANES), F32), pltpu.VMEM((4, tq, V), F32)],
        compiler_params=_params("parallel", "parallel", "parallel"), name="diff_attn_prompt",
    )(lam_qk, q, k, v, sg, subln_g)


def _attn_s_kernel(pt_ref, lq_ref, q_ref, kc_ref, vc_ref, kn_ref, vn_ref, sg_ref, sub_ref, o_ref,
                   q_scr, m_scr, l_scr, acc_scr, *, n_pages, page, KVH, TS, lam_init):
    p = pl.program_id(1)
    V = 2 * DIFF_HEAD
    nr = 4 * TS

    @pl.when(p == 0)
    def _init():
        lane = lax.broadcasted_iota(jnp.int32, (1, V), 1)
        q = q_ref[...]
        for h in range(KVH):
            parts = []
            for m in range(2):
                for g in range(2):
                    qg = q[:, (2 * h + g) * V:(2 * h + g + 1) * V]
                    parts.append(jnp.where((lane < DIFF_HEAD) == (m == 0), qg, 0.0))
            q_scr[h] = jnp.concatenate(parts, axis=0).astype(BF16)
        m_scr[...] = jnp.full(m_scr.shape, NEG_BIG, F32)
        l_scr[...] = jnp.zeros(l_scr.shape, F32)
        acc_scr[...] = jnp.zeros(acc_scr.shape, F32)

    @pl.when(p < n_pages)
    def _past():
        for h in range(KVH):
            kb = kc_ref[pl.ds(h, page, stride=KVH), :].astype(BF16)
            vb = vc_ref[pl.ds(h, page, stride=KVH), :].astype(BF16)
            _online_update(_dot_nt(q_scr[h], kb), vb, m_scr, l_scr, acc_scr, h)

    @pl.when(p == n_pages)
    def _new():
        row_t = lax.broadcasted_iota(jnp.int32, (nr, TS), 0) % TS
        col = lax.broadcasted_iota(jnp.int32, (nr, TS), 1)
        lam = _lam(lq_ref[...], lam_init)
        sg = sg_ref[...]
        for h in range(KVH):
            kb = kn_ref[:, h * V:(h + 1) * V].astype(BF16)
            vb = vn_ref[:, h * V:(h + 1) * V].astype(BF16)
            s = jnp.where(col <= row_t, _dot_nt(q_scr[h], kb), NEG_BIG)
            _online_update(s, vb, m_scr, l_scr, acc_scr, h)
            o = acc_scr[h] / l_scr[h][:, 0:1]
            for g in range(2):
                o1 = o[g * TS:(g + 1) * TS, :]
                o2 = o[(2 + g) * TS:(3 + g) * TS, :]
                sl = slice((2 * h + g) * V, (2 * h + g + 1) * V)
                o_ref[:, sl] = _finish_head(o1, o2, lam, sub_ref[...], sg[:, sl], lam_init)


def _attn_sample(q, k_new, v_new, sg, cache_k, cache_v, page_table, lam_qk, subln_g, j, lam_init, DB, TS):
    N, E = q.shape
    V = 2 * DIFF_HEAD
    n_phys, page, KVH, _ = cache_k.shape
    n_pages = page_table.shape[1]
    kc = cache_k.reshape(n_phys, page * KVH, V)
    vc = cache_v.reshape(n_phys, page * KVH, V)
    nr = 4 * TS
    tok = lambda w: pl.BlockSpec((TS, w), lambda b, p, pt: (b, 0))
    cache = pl.BlockSpec((None, page * KVH, V), lambda b, p, pt: (pt[b, jnp.minimum(p, n_pages - 1)], 0, 0))
    grid_spec = pltpu.PrefetchScalarGridSpec(
        num_scalar_prefetch=1, grid=(DB, n_pages + 1),
        in_specs=[pl.BlockSpec((None, 4, DIFF_HEAD), lambda b, p, pt: (j, 0, 0)), tok(E), cache, cache,
                  tok(KVH * V), tok(KVH * V), tok(E), pl.BlockSpec((None, 1, V), lambda b, p, pt: (j, 0, 0))],
        out_specs=tok(E),
        scratch_shapes=[pltpu.VMEM((KVH, nr, V), BF16), pltpu.VMEM((KVH, nr, LANES), F32),
                        pltpu.VMEM((KVH, nr, LANES), F32), pltpu.VMEM((KVH, nr, V), F32)])
    return pl.pallas_call(
        functools.partial(_attn_s_kernel, n_pages=n_pages, page=page, KVH=KVH, TS=TS, lam_init=lam_init),
        grid_spec=grid_spec, out_shape=jax.ShapeDtypeStruct((N, E), F32),
        compiler_params=_params("parallel", "arbitrary"), name="diff_attn_sample",
    )(page_table, lam_qk, q, kc, vc, k_new, v_new, sg, subln_g)


def _run_group(x3, pos, rows, mods, mods_kv, shift0, wkv0, past, W, depth, n_a):
    nb, R, D = x3.shape
    x = x3.reshape(nb * R, D)
    alpha = (2 * depth) ** 0.25
    m2 = mods.reshape(mods.shape[0], mods.shape[1], 3 * D)
    b0 = rows.mod_base
    new_wkv, new_shift = [], []
    for l in range(n_a):
        proj = _rwkv_proj(x, mods, shift0[l].reshape(nb, 1, D), l, W, rows, W["ones_bd"])
        x_last = x.reshape(nb, R, D)[:, R - 1, :]
        new_shift.append(_mod_rows(x_last, m2[l, b0:b0 + nb, :D], m2[l, b0:b0 + nb, D:2 * D]))
        o, s_last = _rwkv_scan(proj, wkv0, l, W, nb, R)
        new_wkv.append(s_last)
        x = _outproj(o, x, mods, W["w_o_a"], l, W["ln_g"], W["ln_b"], l, rows, alpha)
    n_tab = R if R >= rows.tm else rows.tm
    tabs = _rope_tables(pos, n_tab)
    k_new, v_new = _shared_kv(x, mods_kv, W["w_kv"], tabs, rows)
    for l in range(n_a, depth):
        j = l - n_a
        lam_init = 0.8 - 0.6 * math.exp(-0.3 * l)
        q, sg = _qg_proj(x, mods, W["w_qg"], j, l, tabs, rows)
        if past is None:
            o = _attn_prompt(q, k_new, v_new, sg, W["lam_qk"], W["subln_g"], j, lam_init, nb, R)
        else:
            o = _attn_sample(q, k_new, v_new, sg, past[0], past[1], past[2], W["lam_qk"], W["subln_g"], j, lam_init, nb, R)
        x = _outproj(o, x, mods, W["w_o_b"], j, W["ln_g"], W["ln_b"], l, rows, alpha)
    KVH = k_new.shape[1] // (2 * DIFF_HEAD)
    kv_shape = (nb, R, KVH, 2 * DIFF_HEAD)
    return (x.reshape(nb, R, D), k_new.reshape(kv_shape), v_new.reshape(kv_shape), jnp.stack(new_wkv), jnp.stack(new_shift))


def kernel(x_prompt, x_sample, cache_k, cache_v, page_table, state_wkv, state_shift, c_prompt, c_sample, ada_w, ada_b, ln_g, ln_b, mu, w_rkvg, w0, w_decay1, w_decay2, a0, w_a1, w_a2, k_k, k_a, r_k, gn_g, gn_b, w_o_a, ada_kv_w, ada_kv_b, w_kv, w_qg, lam_qk, subln_g, w_o_b):
    B, T, D = x_prompt.shape
    DB, TS, _ = x_sample.shape
    depth = ada_w.shape[0]
    n_a = w_rkvg.shape[0]
    E = w_rkvg.shape[-1]
    vecE = lambda a: a.reshape(n_a, 1, E)
    lane_grp = jnp.arange(LANES) // RWKV_HEAD
    W = {
        "mu": mu, "w_rkvg": w_rkvg.astype(BF16), "w0": vecE(w0), "w_decay1": w_decay1.astype(BF16),
        "w_decay2": w_decay2.astype(BF16), "a0": vecE(a0), "w_a1": w_a1.astype(BF16), "w_a2": w_a2.astype(BF16),
        "k_k": vecE(k_k), "k_a": vecE(k_a), "r_k": vecE(r_k), "gn_g": vecE(gn_g), "gn_b": vecE(gn_b),
        "w_o_a": w_o_a.astype(BF16), "w_kv": w_kv.astype(BF16), "w_qg": w_qg.astype(BF16),
        "w_o_b": w_o_b.astype(BF16), "lam_qk": lam_qk, "subln_g": subln_g.reshape(-1, 1, 2 * DIFF_HEAD),
        "ln_g": ln_g.reshape(depth, 1, D), "ln_b": ln_b.reshape(depth, 1, D),
        "ones_bd": (lane_grp[:, None] == lane_grp[None, :]).astype(BF16),
    }
    c_all = jnp.concatenate([c_sample, c_prompt], axis=0)
    nbt = DB + B
    mods = _ada(c_all, ada_w, ada_b).reshape(depth, nbt, 1, 3 * D)
    mods_kv = _ada(c_all, ada_kv_w[None], ada_kv_b[None]).reshape(1, nbt, 1, 2 * D)

    tm_p = min(T, 256)
    tm_s = min(DB * TS, 256)
    rows_p = _Rows(B, T, tm_p, DB)
    rows_s = _Rows(DB, TS, tm_s, 0)
    past_len = page_table.shape[1] * cache_k.shape[1]
    pos_p = jnp.arange(T, dtype=jnp.int32)
    pos_s = past_len + jnp.arange(TS, dtype=jnp.int32)
    zeros_shift = jnp.zeros((n_a, B, D), F32)
    y_p, k_p, v_p, wkv_p, shift_p = _run_group(x_prompt, pos_p, rows_p, mods, mods_kv, zeros_shift, None, None, W, depth, n_a)
    y_s, k_s, v_s, wkv_s, shift_s = _run_group(x_sample, pos_s, rows_s, mods, mods_kv, state_shift, state_wkv,
                                               (cache_k, cache_v, page_table), W, depth, n_a)
    return (y_p, y_s, k_p, v_p, k_s, v_s, wkv_p, shift_p, wkv_s, shift_s)
```
